```python
import jax, jax.numpy as jnp
from jax import lax
import numpy as np

D_MODEL = 1024
BATCH = 4
SEQ = 8192
DEPTH = 1

CHUNK = 64
D_MIX = D_MODEL
HEAD_DIM = 64
N_HEADS_A = 8
N_HEADS_B = 8
D_A = N_HEADS_A * HEAD_DIM
D_B = N_HEADS_B * HEAD_DIM
SG_BLOCK = 128
Q_BLOCK = 128
D_FF = 2816
CONV_W = 3
EPS = 1e-6
D_IN = 2 * D_A + 3 * D_B + N_HEADS_B

kernel_name = "hymba_gmlp_fox_convffn"


def rms_norm(x, g):
    xf = x.astype(jnp.float32)
    y = xf * lax.rsqrt(jnp.mean(xf * xf, axis=-1, keepdims=True) + EPS)
    return (y * g.astype(jnp.float32)).astype(x.dtype)


def spatial_gating(u, v, ln_g, w_s, b_s):
    B, S, _ = u.shape
    n = S // SG_BLOCK
    v = v.reshape(B, n, SG_BLOCK, N_HEADS_A, HEAD_DIM)
    vf = v.astype(jnp.float32)
    mu = jnp.mean(vf, axis=-1, keepdims=True)
    var = jnp.mean(jnp.square(vf - mu), axis=-1, keepdims=True)
    vn = ((vf - mu) * lax.rsqrt(var + EPS) * ln_g.astype(jnp.float32)).astype(u.dtype)
    pos_chunk = jnp.arange(SG_BLOCK) // CHUNK
    mask = pos_chunk[:, None] >= pos_chunk[None, :]
    w = jnp.where(mask[None], w_s, 0).astype(u.dtype)
    mixed = jnp.einsum('hts,bnshd->bnthd', w, vn) + b_s.T.astype(u.dtype)[None, None, :, :, None]
    out = u.reshape(B, n, SG_BLOCK, N_HEADS_A, HEAD_DIM) * mixed
    return out.reshape(B, S, D_A)


def forgetting_attention(q, k, v, f_logit):
    B, S, H, Dh = q.shape
    n = S // Q_BLOCK
    scale = Dh ** -0.5
    c = jnp.cumsum(jax.nn.log_sigmoid(f_logit.astype(jnp.float32)), axis=1)
    c = c.transpose(0, 2, 1)
    qb = q.reshape(B, n, Q_BLOCK, H, Dh).transpose(1, 0, 3, 2, 4)
    cb = c.reshape(B, H, n, Q_BLOCK).transpose(2, 0, 1, 3)
    kpos = jnp.arange(S)

    def block(args):
        i, qi, ci = args
        s = jnp.einsum('bhtd,bshd->bhts', qi, k, preferred_element_type=jnp.float32) * scale
        s = s + ci[..., :, None] - c[:, :, None, :]
        qpos = i * Q_BLOCK + jnp.arange(Q_BLOCK)
        s = jnp.where(kpos[None, :] <= qpos[:, None], s, -jnp.inf)
        p = jax.nn.softmax(s, axis=-1)
        return jnp.einsum('bhts,bshd->bthd', p.astype(v.dtype), v)

    out = lax.map(block, (jnp.arange(n), qb, cb))
    return out.transpose(1, 0, 2, 3, 4).reshape(B, S, H * Dh)


def conv_ffn(h, w_up, w_conv, b_conv, w_down):
    a = h @ w_up
    C = a.shape[-1]
    a = lax.conv_general_dilated(
        a, w_conv[:, None, :].astype(a.dtype), window_strides=(1,),
        padding=[(CONV_W - 1, 0)], dimension_numbers=('NWC', 'WIO', 'NWC'),
        feature_group_count=C) + b_conv.astype(a.dtype)
    g, val = jnp.split(a, 2, axis=-1)
    return (jax.nn.silu(g) * val) @ w_down


def setup_inputs(seed: int = 0) -> dict:
    key = jax.random.key(seed)
    ks = jax.random.split(key, 14)
    L = DEPTH
    nrm = jax.random.normal
    return {
        "x": nrm(ks[0], (BATCH, SEQ, D_MODEL), jnp.float32),
        "norm_mix_g": 1.0 + 0.02 * nrm(ks[1], (L, D_MODEL), jnp.float32),
        "w_in": nrm(ks[2], (L, D_MODEL, D_IN), jnp.float32) * D_MODEL ** -0.5,
        "f_bias": 3.0 + 0.5 * nrm(ks[3], (L, N_HEADS_B), jnp.float32),
        "sg_ln_g": 1.0 + 0.02 * nrm(ks[4], (L, N_HEADS_A, HEAD_DIM), jnp.float32),
        "sg_w": nrm(ks[5], (L, N_HEADS_A, SG_BLOCK, SG_BLOCK), jnp.float32) * SG_BLOCK ** -0.5,
        "sg_b": 1.0 + 0.1 * nrm(ks[6], (L, N_HEADS_A, SG_BLOCK), jnp.float32),
        "w_out": nrm(ks[7], (L, D_MIX, D_MODEL), jnp.float32) * D_MIX ** -0.5,
        "norm_ffn_g": 1.0 + 0.02 * nrm(ks[8], (L, D_MODEL), jnp.float32),
        "w_up": nrm(ks[9], (L, D_MODEL, 2 * D_FF), jnp.float32) * D_MODEL ** -0.5,
        "w_conv": nrm(ks[10], (L, CONV_W, 2 * D_FF), jnp.float32) * CONV_W ** -0.5,
        "b_conv": 0.02 * nrm(ks[11], (L, 2 * D_FF), jnp.float32),
        "w_down": nrm(ks[12], (L, D_FF, D_MODEL), jnp.float32) * D_FF ** -0.5,
        "norm_final_g": 1.0 + 0.02 * nrm(ks[13], (D_MODEL,), jnp.float32),
    }


def reference(x, norm_mix_g, w_in, f_bias, sg_ln_g, sg_w, sg_b, w_out,
              norm_ffn_g, w_up, w_conv, b_conv, w_down, norm_final_g):
    B, S, _ = x.shape
    for l in range(DEPTH):
        h = rms_norm(x, norm_mix_g[l])
        z = h @ w_in[l]
        o = 0
        u_a = jax.nn.gelu(z[..., o:o + D_A], approximate=False); o += D_A
        v_a = jax.nn.gelu(z[..., o:o + D_A], approximate=False); o += D_A
        q_b = z[..., o:o + D_B].reshape(B, S, N_HEADS_B, HEAD_DIM); o += D_B
        k_b = z[..., o:o + D_B].reshape(B, S, N_HEADS_B, HEAD_DIM); o += D_B
        v_b = z[..., o:o + D_B].reshape(B, S, N_HEADS_B, HEAD_DIM); o += D_B
        f_logit = z[..., o:o + N_HEADS_B] + f_bias[l].astype(z.dtype)
        out_a = spatial_gating(u_a, v_a, sg_ln_g[l], sg_w[l], sg_b[l])
        out_b = forgetting_attention(q_b, k_b, v_b, f_logit)
        x = x + jnp.concatenate([out_a, out_b], axis=-1) @ w_out[l]
        x = x + conv_ffn(rms_norm(x, norm_ffn_g[l]), w_up[l], w_conv[l], b_conv[l], w_down[l])
    return rms_norm(x, norm_final_g)
```

```python
import functools
import math

import jax
import jax.numpy as jnp
import numpy as np
from jax import lax
from jax.experimental import pallas as pl
from jax.experimental.pallas import tpu as pltpu

HEAD_DIM = 64
N_HEADS_A = 8
N_HEADS_B = 8
SG_BLOCK = 128
SG_CHUNK = 64
CONV_W = 3
EPS = 1e-6

V7X_LANES = 128
BF16_SUBLANES = 16
V7X_VMEM_LIMIT = 56 * 1024 * 1024

PROJ_TOKENS = 512
FFN_TOKENS = 512
FFN_CHUNK = 256
ATT_BLOCK = 256
QK_ROWS = 2 * HEAD_DIM
GATE_ROWS = BF16_SUBLANES

_NT = (((1,), (1,)), ((), ()))
_TN = (((0,), (0,)), ((), ()))
_F32 = jnp.float32
_BF16 = jnp.bfloat16


def _split3(x):
    hi = x.astype(_BF16)
    r1 = x - hi.astype(_F32)
    mid = r1.astype(_BF16)
    lo = (r1 - mid.astype(_F32)).astype(_BF16)
    return hi, mid, lo


def _rms_norm(x, g):
    return x * lax.rsqrt(jnp.mean(x * x, axis=-1, keepdims=True) + EPS) * g


def _gelu(x):
    return 0.5 * x * (1.0 + lax.erf(x * (1.0 / math.sqrt(2.0))))


def _mixer_in_kernel(x_ref, g_ref, wuv_ref, wqk_ref, wvb_ref, wf_ref, fb_ref, lng_ref, wsg_ref,
                     bsg_ref, triu_ref, place_ref, outa_ref, q_ref, k_ref, v_ref, carry_ref):
    tm = x_ref.shape[1]
    n_sg = tm // SG_BLOCK
    n_att = tm // ATT_BLOCK
    d_a = N_HEADS_A * HEAD_DIM
    d_b = N_HEADS_B * HEAD_DIM

    @pl.when(pl.program_id(1) == 0)
    def _():
        carry_ref[...] = jnp.zeros_like(carry_ref)

    h = _rms_norm(x_ref[0], g_ref[...]).astype(_BF16)

    uv = lax.dot_general(wuv_ref[...], h, _NT, preferred_element_type=_F32)
    u = _gelu(uv[:d_a])
    v = _gelu(uv[d_a:]).reshape(N_HEADS_A, HEAD_DIM, tm)
    mu = jnp.mean(v, axis=1, keepdims=True)
    vc = v - mu
    var = jnp.mean(vc * vc, axis=1, keepdims=True)
    vn = (vc * lax.rsqrt(var + EPS)).reshape(d_a, tm) * lng_ref[...]
    vn = vn.astype(_BF16)

    row_chunk = lax.broadcasted_iota(jnp.int32, (SG_BLOCK, SG_BLOCK), 0) // SG_CHUNK
    col_chunk = lax.broadcasted_iota(jnp.int32, (SG_BLOCK, SG_BLOCK), 1) // SG_CHUNK
    sg_mask = row_chunk >= col_chunk
    for hd in range(N_HEADS_A):
        rows = slice(hd * HEAD_DIM, (hd + 1) * HEAD_DIM)
        w = jnp.where(sg_mask, wsg_ref[hd], 0.0).astype(_BF16)
        lhs = jnp.concatenate(
            [vn[rows, j * SG_BLOCK:(j + 1) * SG_BLOCK] for j in range(n_sg)], axis=0)
        mixed = lax.dot_general(lhs, w, _NT, preferred_element_type=_F32)
        mixed = mixed + bsg_ref[hd]
        for j in range(n_sg):
            cols = slice(j * SG_BLOCK, (j + 1) * SG_BLOCK)
            outa_ref[0, rows, cols] = (
                u[rows, cols] * mixed[j * HEAD_DIM:(j + 1) * HEAD_DIM]).astype(outa_ref.dtype)

    f = lax.dot_general(wf_ref[...], h, _NT, preferred_element_type=_F32) + fb_ref[...]
    ls = jnp.minimum(f, 0.0) - jnp.log1p(jnp.exp(-jnp.abs(f)))
    c = carry_ref[:, :1] + sum(
        jnp.dot(part, triu_ref[...], preferred_element_type=_F32) for part in _split3(ls))
    carry_ref[...] = jnp.broadcast_to(c[:, tm - 1:], carry_ref.shape)
    c_parts = jnp.concatenate(_split3(c) + (jnp.ones((GATE_ROWS, tm), _BF16),), axis=0)
    bias = jnp.dot(place_ref[...], c_parts, preferred_element_type=_F32)

    qk = lax.dot_general(wqk_ref[...], h, _NT, preferred_element_type=_F32)
    vb = lax.dot_general(wvb_ref[...], h, _NT, preferred_element_type=_F32)
    for hd in range(N_HEADS_B):
        rows = slice(hd * HEAD_DIM, (hd + 1) * HEAD_DIM)
        krows = slice(d_b + hd * HEAD_DIM, d_b + (hd + 1) * HEAD_DIM)
        for j in range(n_att):
            cols = slice(j * ATT_BLOCK, (j + 1) * ATT_BLOCK)
            q_ref[0, hd, j, :HEAD_DIM, :] = qk[rows, cols].astype(q_ref.dtype)
            q_ref[0, hd, j, HEAD_DIM:, :] = bias[rows, cols].astype(q_ref.dtype)
            k_ref[0, hd, j, :HEAD_DIM, :] = qk[krows, cols].astype(k_ref.dtype)
            k_ref[0, hd, j, HEAD_DIM:, :] = bias[krows, cols].astype(k_ref.dtype)
            v_ref[0, hd, j] = vb[rows, cols].astype(v_ref.dtype)


def _mixer_in(x, norm_g, w_in, f_bias, sg_ln_g, sg_w, sg_b):
    batch, seq, d_model = x.shape
    tm = PROJ_TOKENS
    d_a = N_HEADS_A * HEAD_DIM
    d_b = N_HEADS_B * HEAD_DIM
    n_blk = seq // ATT_BLOCK
    scale = HEAD_DIM ** -0.5

    o = 0
    wuv = w_in[:, o:o + 2 * d_a].T.astype(_BF16); o += 2 * d_a
    wq = w_in[:, o:o + d_b] * scale; o += d_b
    wk = w_in[:, o:o + d_b]; o += d_b
    wqk = jnp.concatenate([wq, wk], axis=1).T.astype(_BF16)
    wvb = w_in[:, o:o + d_b].T.astype(_BF16); o += d_b
    pad_rows = ((0, GATE_ROWS - N_HEADS_B), (0, 0))
    wf = jnp.pad(w_in[:, o:o + N_HEADS_B].T, pad_rows).astype(_BF16)
    fb = jnp.broadcast_to(jnp.pad(f_bias.reshape(N_HEADS_B, 1), pad_rows), (GATE_ROWS, tm))
    lng = jnp.broadcast_to(sg_ln_g.reshape(d_a, 1), (d_a, tm))
    bsg = sg_b.reshape(N_HEADS_A, 1, SG_BLOCK)
    triu = jnp.asarray(np.triu(np.ones((tm, tm), np.float32)), _BF16)

    place = np.zeros((2 * d_b, 4 * GATE_ROWS), np.float32)
    ones_col = 3 * GATE_ROWS
    for hd in range(N_HEADS_B):
        qr, kr = hd * HEAD_DIM, d_b + hd * HEAD_DIM
        for t in range(3):
            place[qr + t, ones_col] = 1.0
            place[qr + 3 + t, t * GATE_ROWS + hd] = 1.0
            place[kr + t, t * GATE_ROWS + hd] = -1.0
            place[kr + 3 + t, ones_col] = 1.0
    place = jnp.asarray(place, _BF16)

    const2 = lambda shape: pl.BlockSpec(shape, lambda b, s: (0, 0))
    const3 = lambda shape: pl.BlockSpec(shape, lambda b, s: (0, 0, 0))
    blk5 = lambda rows: pl.BlockSpec((1, N_HEADS_B, tm // ATT_BLOCK, rows, ATT_BLOCK),
                                     lambda b, s: (b, 0, s, 0, 0))
    return pl.pallas_call(
        _mixer_in_kernel,
        grid=(batch, seq // tm),
        in_specs=[
            pl.BlockSpec((1, tm, d_model), lambda b, s: (b, s, 0)),
            const2((1, d_model)),
            const2(wuv.shape), const2(wqk.shape), const2(wvb.shape), const2(wf.shape),
            const2(fb.shape), const2(lng.shape), const3(sg_w.shape), const3(bsg.shape),
            const2(triu.shape), const2(place.shape),
        ],
        out_specs=[
            pl.BlockSpec((1, d_a, tm), lambda b, s: (b, 0, s)),
            blk5(QK_ROWS), blk5(QK_ROWS), blk5(HEAD_DIM),
        ],
        out_shape=[
            jax.ShapeDtypeStruct((batch, d_a, seq), _BF16),
            jax.ShapeDtypeStruct((batch, N_HEADS_B, n_blk, QK_ROWS, ATT_BLOCK), _BF16),
            jax.ShapeDtypeStruct((batch, N_HEADS_B, n_blk, QK_ROWS, ATT_BLOCK), _BF16),
            jax.ShapeDtypeStruct((batch, N_HEADS_B, n_blk, HEAD_DIM, ATT_BLOCK), _BF16),
        ],
        scratch_shapes=[pltpu.VMEM((GATE_ROWS, V7X_LANES), _F32)],
        compiler_params=pltpu.CompilerParams(
            dimension_semantics=("arbitrary", "arbitrary"), vmem_limit_bytes=V7X_VMEM_LIMIT),
        name="mixer_in",
    )(x, norm_g.reshape(1, d_model), wuv, wqk, wvb, wf, fb, lng, sg_w, bsg, triu, place)


def _fox_kernel(q_ref, k_ref, v_ref, o_ref, ktok_ref):
    n_blk = q_ref.shape[2]
    blk = q_ref.shape[4]

    def to_token_major(j, _):
        ktok_ref[j] = k_ref[0, 0, j].astype(_F32).T.astype(_BF16)
        return 0
    lax.fori_loop(0, n_blk, to_token_major, 0)

    s_idx = lax.broadcasted_iota(jnp.int32, (blk, blk), 0)
    t_idx = lax.broadcasted_iota(jnp.int32, (blk, blk), 1)
    causal = s_idx <= t_idx

    def q_block(i, _):
        q = q_ref[0, 0, i]

        def update(j, carry, masked):
            m, l, acc = carry
            s = jnp.dot(ktok_ref[j], q, preferred_element_type=_F32)
            if masked:
                s = jnp.where(causal, s, -jnp.inf)
            m_new = jnp.maximum(m, jnp.max(s, axis=0, keepdims=True))
            alpha = jnp.exp(m - m_new)
            p = jnp.exp(s - m_new)
            l = alpha * l + jnp.sum(p, axis=0, keepdims=True)
            acc = alpha * acc + jnp.dot(v_ref[0, 0, j], p.astype(_BF16),
                                        preferred_element_type=_F32)
            return m_new, l, acc

        init = (jnp.full((1, blk), -1e30, _F32), jnp.zeros((1, blk), _F32),
                jnp.zeros((HEAD_DIM, blk), _F32))
        carry = lax.fori_loop(0, i, lambda j, c: update(j, c, False), init)
        _, l, acc = update(i, carry, True)
        o_ref[0, 0, i] = (acc / l).astype(o_ref.dtype)
        return 0

    lax.fori_loop(0, n_blk, q_block, 0)


def _fox_attention(q, k, v):
    batch, heads, n_blk, _, blk = q.shape
    spec = lambda rows: pl.BlockSpec((1, 1, n_blk, rows, blk), lambda b, h: (b, h, 0, 0, 0))
    return pl.pallas_call(
        _fox_kernel,
        grid=(batch, heads),
        in_specs=[spec(QK_ROWS), spec(QK_ROWS), spec(HEAD_DIM)],
        out_specs=spec(HEAD_DIM),
        out_shape=jax.ShapeDtypeStruct((batch, heads, n_blk, HEAD_DIM, blk), _BF16),
        scratch_shapes=[pltpu.VMEM((n_blk, blk, QK_ROWS), _BF16)],
        compiler_params=pltpu.CompilerParams(
            dimension_semantics=("arbitrary", "arbitrary"), vmem_limit_bytes=V7X_VMEM_LIMIT),
        name="fox_attention",
    )(q, k, v)


def _ffn_kernel(x_ref, a_ref, b_ref, wo_ref, g2_ref, wup_ref, wc_ref, bc_ref, wd_ref, g3_ref,
                o_ref, tail_ref):
    tm = x_ref.shape[1]
    n_chunks = wup_ref.shape[0]
    fc = wd_ref.shape[1]
    n_att = b_ref.shape[2]

    @pl.when(pl.program_id(1) == 0)
    def _():
        tail_ref[...] = jnp.zeros_like(tail_ref)

    mix_a = a_ref[0]
    mix_b = jnp.concatenate(
        [jnp.concatenate([b_ref[0, hd, j] for j in range(n_att)], axis=1)
         for hd in range(N_HEADS_B)], axis=0)
    mix = jnp.concatenate([mix_a, mix_b], axis=0)
    y = lax.dot_general(mix, wo_ref[...], _TN, preferred_element_type=_F32)
    x1 = x_ref[0] + y
    h = _rms_norm(x1, g2_ref[...]).astype(_BF16)

    first_rows = lax.broadcasted_iota(jnp.int32, (8, 2 * fc), 0)
    acc = jnp.zeros_like(x1)
    for c in range(n_chunks):
        a = jnp.dot(h, wup_ref[c], preferred_element_type=_F32)
        tail = tail_ref[c]
        tail_ref[c] = a[tm - 8:]
        prev1 = pltpu.roll(a, 1, 0)
        prev2 = pltpu.roll(a, 2, 0)
        head1 = jnp.where(first_rows < 1, pltpu.roll(tail, 1, 0), prev1[:8])
        head2 = jnp.where(first_rows < 2, pltpu.roll(tail, 2, 0), prev2[:8])
        prev1 = jnp.concatenate([head1, prev1[8:]], axis=0)
        prev2 = jnp.concatenate([head2, prev2[8:]], axis=0)
        wc = wc_ref[c]
        conv = prev2 * wc[0:1] + prev1 * wc[1:2] + a * wc[2:3] + bc_ref[c]
        gate, val = conv[:, :fc], conv[:, fc:]
        act = (gate * (1.0 / (1.0 + jnp.exp(-gate))) * val).astype(_BF16)
        acc = acc + jnp.dot(act, wd_ref[c], preferred_element_type=_F32)
    x2 = x1 + acc
    o_ref[0] = _rms_norm(x2, g3_ref[...]).astype(o_ref.dtype)


def _ffn(x, mix_a, mix_b, w_out, g2, w_up, w_conv, b_conv, w_down, g3):
    batch, seq, d_model = x.shape
    tm = FFN_TOKENS
    fc = FFN_CHUNK
    d_ff = w_down.shape[0]
    n_chunks = d_ff // fc
    d_a = mix_a.shape[1]

    def chunked(w):
        lead = w.shape[:-1]
        g = w[..., :d_ff].reshape(lead + (n_chunks, fc))
        v = w[..., d_ff:].reshape(lead + (n_chunks, fc))
        return jnp.moveaxis(jnp.concatenate([g, v], axis=-1), -2, 0)
    wup = chunked(w_up).astype(_BF16)
    wc = chunked(w_conv)
    bc = chunked(b_conv.reshape(1, -1))
    wd = w_down.reshape(n_chunks, fc, d_model).astype(_BF16)
    wo = w_out.astype(_BF16)

    n_att = tm // ATT_BLOCK
    const2 = lambda shape: pl.BlockSpec(shape, lambda b, s: (0, 0))
    const3 = lambda shape: pl.BlockSpec(shape, lambda b, s: (0, 0, 0))
    return pl.pallas_call(
        _ffn_kernel,
        grid=(batch, seq // tm),
        in_specs=[
            pl.BlockSpec((1, tm, d_model), lambda b, s: (b, s, 0)),
            pl.BlockSpec((1, d_a, tm), lambda b, s: (b, 0, s)),
            pl.BlockSpec((1, N_HEADS_B, n_att, HEAD_DIM, ATT_BLOCK), lambda b, s: (b, 0, s, 0, 0)),
            const2(wo.shape), const2((1, d_model)),
            const3(wup.shape), const3(wc.shape), const3(bc.shape), const3(wd.shape),
            const2((1, d_model)),
        ],
        out_specs=pl.BlockSpec((1, tm, d_model), lambda b, s: (b, s, 0)),
        out_shape=jax.ShapeDtypeStruct(x.shape, x.dtype),
        scratch_shapes=[pltpu.VMEM((n_chunks, 8, 2 * fc), _F32)],
        compiler_params=pltpu.CompilerParams(
            dimension_semantics=("arbitrary", "arbitrary"), vmem_limit_bytes=V7X_VMEM_LIMIT),
        name="out_proj_ffn",
    )(x, mix_a, mix_b, wo, g2.reshape(1, d_model), wup, wc, bc, wd, g3.reshape(1, d_model))


def kernel(x, norm_mix_g, w_in, f_bias, sg_ln_g, sg_w, sg_b, w_out, norm_ffn_g, w_up, w_conv,
           b_conv, w_down, norm_final_g):
    assert w_in.shape[0] == 1, "single trunk layer only"
    mix_a, q, k, v = _mixer_in(x, norm_mix_g[0], w_in[0], f_bias[0], sg_ln_g[0], sg_w[0], sg_b[0])
    mix_b = _fox_attention(q, k, v)
    return _ffn(x, mix_a, mix_b, w_out[0], norm_ffn_g[0], w_up[0], w_conv[0], b_conv[0], w_down[0],
                norm_final_g)
```

```python
import functools
import math

import jax
import jax.numpy as jnp
import numpy as np
from jax import lax
from jax.experimental import pallas as pl
from jax.experimental.pallas import tpu as pltpu

HEAD_DIM = 64
N_HEADS_A = 8
N_HEADS_B = 8
SG_BLOCK = 128
SG_CHUNK = 64
CONV_W = 3
EPS = 1e-6
LOG2_E = math.log2(math.e)

V7X_LANES = 128
BF16_SUBLANES = 16
V7X_VMEM_LIMIT = 56 * 1024 * 1024

PROJ_TOKENS = 512
FFN_TOKENS = 512
FFN_CHUNK = 256
ATT_BLOCK = 256
ATT_Q_BLOCKS = 4
V_ROWS = HEAD_DIM + BF16_SUBLANES
QK_ROWS = 2 * HEAD_DIM
GATE_ROWS = BF16_SUBLANES

_NT = (((1,), (1,)), ((), ()))
_TN = (((0,), (0,)), ((), ()))
_F32 = jnp.float32
_BF16 = jnp.bfloat16


def _split3(x):
    hi = x.astype(_BF16)
    r1 = x - hi.astype(_F32)
    mid = r1.astype(_BF16)
    lo = (r1 - mid.astype(_F32)).astype(_BF16)
    return hi, mid, lo


def _rms_norm(x, g):
    return x * lax.rsqrt(jnp.mean(x * x, axis=-1, keepdims=True) + EPS) * g


def _gelu(x):
    return 0.5 * x * (1.0 + lax.erf(x * (1.0 / math.sqrt(2.0))))


def _mixer_in_kernel(x_ref, g_ref, wuv_ref, wqk_ref, wvb_ref, wf_ref, fb_ref, lng_ref, wsg_ref,
                     bsg_ref, triu_ref, place_ref, outa_ref, q_ref, k_ref, v_ref, carry_ref):
    tm = x_ref.shape[1]
    n_sg = tm // SG_BLOCK
    n_att = tm // ATT_BLOCK
    d_a = N_HEADS_A * HEAD_DIM
    d_b = N_HEADS_B * HEAD_DIM

    @pl.when(pl.program_id(1) == 0)
    def _():
        carry_ref[...] = jnp.zeros_like(carry_ref)

    h = _rms_norm(x_ref[0], g_ref[...]).astype(_BF16)

    uv = lax.dot_general(wuv_ref[...], h, _NT, preferred_element_type=_F32)
    u = _gelu(uv[:d_a])
    v = _gelu(uv[d_a:]).reshape(N_HEADS_A, HEAD_DIM, tm)
    mu = jnp.mean(v, axis=1, keepdims=True)
    vc = v - mu
    var = jnp.mean(vc * vc, axis=1, keepdims=True)
    vn = (vc * lax.rsqrt(var + EPS)).reshape(d_a, tm) * lng_ref[...]
    vn = vn.astype(_BF16)

    row_chunk = lax.broadcasted_iota(jnp.int32, (SG_BLOCK, SG_BLOCK), 0) // SG_CHUNK
    col_chunk = lax.broadcasted_iota(jnp.int32, (SG_BLOCK, SG_BLOCK), 1) // SG_CHUNK
    sg_mask = row_chunk >= col_chunk
    for hd in range(N_HEADS_A):
        rows = slice(hd * HEAD_DIM, (hd + 1) * HEAD_DIM)
        w = jnp.where(sg_mask, wsg_ref[hd], 0.0).astype(_BF16)
        lhs = jnp.concatenate(
            [vn[rows, j * SG_BLOCK:(j + 1) * SG_BLOCK] for j in range(n_sg)], axis=0)
        mixed = lax.dot_general(lhs, w, _NT, preferred_element_type=_F32)
        mixed = mixed + bsg_ref[hd]
        for j in range(n_sg):
            cols = slice(j * SG_BLOCK, (j + 1) * SG_BLOCK)
            outa_ref[0, rows, cols] = (
                u[rows, cols] * mixed[j * HEAD_DIM:(j + 1) * HEAD_DIM]).astype(outa_ref.dtype)

    f = lax.dot_general(wf_ref[...], h, _NT, preferred_element_type=_F32) + fb_ref[...]
    ls = jnp.minimum(f, 0.0) - jnp.log1p(jnp.exp(-jnp.abs(f)))
    c = carry_ref[:, :1] + sum(
        jnp.dot(part, triu_ref[...], preferred_element_type=_F32) for part in _split3(ls))
    carry_ref[...] = jnp.broadcast_to(c[:, tm - 1:], carry_ref.shape)
    c_parts = jnp.concatenate(_split3(c * LOG2_E) + (jnp.ones((GATE_ROWS, tm), _BF16),), axis=0)
    bias = jnp.dot(place_ref[...], c_parts, preferred_element_type=_F32)

    qk = lax.dot_general(wqk_ref[...], h, _NT, preferred_element_type=_F32)
    vb = lax.dot_general(wvb_ref[...], h, _NT, preferred_element_type=_F32)
    ones_row = (lax.broadcasted_iota(jnp.int32, (V_ROWS - HEAD_DIM, ATT_BLOCK), 0) == 0).astype(v_ref.dtype)
    for hd in range(N_HEADS_B):
        rows = slice(hd * HEAD_DIM, (hd + 1) * HEAD_DIM)
        krows = slice(d_b + hd * HEAD_DIM, d_b + (hd + 1) * HEAD_DIM)
        for j in range(n_att):
            cols = slice(j * ATT_BLOCK, (j + 1) * ATT_BLOCK)
            q_ref[0, hd, j, :HEAD_DIM, :] = qk[rows, cols].astype(q_ref.dtype)
            q_ref[0, hd, j, HEAD_DIM:, :] = bias[rows, cols].astype(q_ref.dtype)
            k_ref[0, hd, j, :HEAD_DIM, :] = qk[krows, cols].astype(k_ref.dtype)
            k_ref[0, hd, j, HEAD_DIM:, :] = bias[krows, cols].astype(k_ref.dtype)
            v_ref[0, hd, j, :HEAD_DIM, :] = vb[rows, cols].astype(v_ref.dtype)
            v_ref[0, hd, j, HEAD_DIM:, :] = ones_row


def _mixer_in(x, norm_g, w_in, f_bias, sg_ln_g, sg_w, sg_b):
    batch, seq, d_model = x.shape
    tm = PROJ_TOKENS
    d_a = N_HEADS_A * HEAD_DIM
    d_b = N_HEADS_B * HEAD_DIM
    n_blk = seq // ATT_BLOCK
    scale = HEAD_DIM ** -0.5 * LOG2_E

    o = 0
    wuv = w_in[:, o:o + 2 * d_a].T.astype(_BF16); o += 2 * d_a
    wq = w_in[:, o:o + d_b] * scale; o += d_b
    wk = w_in[:, o:o + d_b]; o += d_b
    wqk = jnp.concatenate([wq, wk], axis=1).T.astype(_BF16)
    wvb = w_in[:, o:o + d_b].T.astype(_BF16); o += d_b
    pad_rows = ((0, GATE_ROWS - N_HEADS_B), (0, 0))
    wf = jnp.pad(w_in[:, o:o + N_HEADS_B].T, pad_rows).astype(_BF16)
    fb = jnp.broadcast_to(jnp.pad(f_bias.reshape(N_HEADS_B, 1), pad_rows), (GATE_ROWS, tm))
    lng = jnp.broadcast_to(sg_ln_g.reshape(d_a, 1), (d_a, tm))
    bsg = sg_b.reshape(N_HEADS_A, 1, SG_BLOCK)
    triu = jnp.asarray(np.triu(np.ones((tm, tm), np.float32)), _BF16)

    place = np.zeros((2 * d_b, 4 * GATE_ROWS), np.float32)
    ones_col = 3 * GATE_ROWS
    for hd in range(N_HEADS_B):
        qr, kr = hd * HEAD_DIM, d_b + hd * HEAD_DIM
        for t in range(3):
            place[qr + t, ones_col] = 1.0
            place[qr + 3 + t, t * GATE_ROWS + hd] = 1.0
            place[kr + t, t * GATE_ROWS + hd] = -1.0
            place[kr + 3 + t, ones_col] = 1.0
    place = jnp.asarray(place, _BF16)

    const2 = lambda shape: pl.BlockSpec(shape, lambda b, s: (0, 0))
    const3 = lambda shape: pl.BlockSpec(shape, lambda b, s: (0, 0, 0))
    blk5 = lambda rows: pl.BlockSpec((1, N_HEADS_B, tm // ATT_BLOCK, rows, ATT_BLOCK),
                                     lambda b, s: (b, 0, s, 0, 0))
    return pl.pallas_call(
        _mixer_in_kernel,
        grid=(batch, seq // tm),
        in_specs=[
            pl.BlockSpec((1, tm, d_model), lambda b, s: (b, s, 0)),
            const2((1, d_model)),
            const2(wuv.shape), const2(wqk.shape), const2(wvb.shape), const2(wf.shape),
            const2(fb.shape), const2(lng.shape), const3(sg_w.shape), const3(bsg.shape),
            const2(triu.shape), const2(place.shape),
        ],
        out_specs=[
            pl.BlockSpec((1, d_a, tm), lambda b, s: (b, 0, s)),
            blk5(QK_ROWS), blk5(QK_ROWS), blk5(V_ROWS),
        ],
        out_shape=[
            jax.ShapeDtypeStruct((batch, d_a, seq), _BF16),
            jax.ShapeDtypeStruct((batch, N_HEADS_B, n_blk, QK_ROWS, ATT_BLOCK), _BF16),
            jax.ShapeDtypeStruct((batch, N_HEADS_B, n_blk, QK_ROWS, ATT_BLOCK), _BF16),
            jax.ShapeDtypeStruct((batch, N_HEADS_B, n_blk, V_ROWS, ATT_BLOCK), _BF16),
        ],
        scratch_shapes=[pltpu.VMEM((GATE_ROWS, V7X_LANES), _F32)],
        compiler_params=pltpu.CompilerParams(
            dimension_semantics=("arbitrary", "arbitrary"), vmem_limit_bytes=V7X_VMEM_LIMIT),
        name="mixer_in",
    )(x, norm_g.reshape(1, d_model), wuv, wqk, wvb, wf, fb, lng, sg_w, bsg, triu, place)


def _fox_kernel(q_ref, k_ref, v_ref, o_ref, ktok_ref, s_ref, p_ref, acc_ref):
    n_blk = q_ref.shape[2]
    blk = q_ref.shape[4]
    tq = ATT_Q_BLOCKS * blk

    def to_token_major(j, _):
        ktok_ref[j] = k_ref[0, 0, j].astype(_F32).T.astype(_BF16)
        return 0
    lax.fori_loop(0, n_blk, to_token_major, 0)

    s_minus_t = (lax.broadcasted_iota(jnp.int32, (blk, tq), 0)
                 - lax.broadcasted_iota(jnp.int32, (blk, tq), 1))

    def q_block(i, _):
        q = jnp.concatenate([q_ref[0, 0, ATT_Q_BLOCKS * i + r] for r in range(ATT_Q_BLOCKS)], axis=1)
        n_keys = ATT_Q_BLOCKS * (i + 1)

        def scores(j, masked):
            s = jnp.dot(ktok_ref[j], q, preferred_element_type=_F32)
            if masked:
                s = jnp.where(s_minus_t <= (i * ATT_Q_BLOCKS - j) * blk, s, -jnp.inf)
            return s

        def step(j, slot, m, alpha_prev, masked):
            pv = jnp.dot(v_ref[0, 0, jnp.maximum(j - 1, 0)], p_ref[1 - slot], preferred_element_type=_F32)
            acc_ref[...] = alpha_prev * acc_ref[...] + pv
            s = s_ref[slot]
            m_new = jnp.maximum(m, jnp.max(s, axis=0, keepdims=True))
            alpha = jnp.exp2(m - m_new)
            p_ref[slot] = jnp.exp2(s - m_new).astype(_BF16)
            s_ref[1 - slot] = scores(jnp.minimum(j + 1, n_keys - 1), masked)
            return m_new, alpha

        def step_group(g, carry, masked):
            for r in range(ATT_Q_BLOCKS):
                carry = step(ATT_Q_BLOCKS * g + r, r % 2, *carry, masked)
            return carry

        s_ref[0] = scores(0, True)
        p_ref[1] = jnp.zeros(p_ref.shape[1:], p_ref.dtype)
        acc_ref[...] = jnp.zeros_like(acc_ref)
        init = (jnp.full((1, tq), -1e30, _F32), jnp.ones((1, tq), _F32))
        n_plain = jnp.maximum(i - 1, 0)
        carry = lax.fori_loop(0, n_plain, lambda g, c: step_group(g, c, False), init)
        _, alpha = lax.fori_loop(n_plain, i + 1, lambda g, c: step_group(g, c, True), carry)
        acc = alpha * acc_ref[...] + jnp.dot(v_ref[0, 0, n_keys - 1], p_ref[1], preferred_element_type=_F32)
        out = acc[:HEAD_DIM] / acc[HEAD_DIM:HEAD_DIM + 1]
        for r in range(ATT_Q_BLOCKS):
            o_ref[0, 0, ATT_Q_BLOCKS * i + r] = out[:, r * blk:(r + 1) * blk].astype(o_ref.dtype)
        return 0

    lax.fori_loop(0, n_blk // ATT_Q_BLOCKS, q_block, 0)


def _fox_attention(q, k, v):
    batch, heads, n_blk, _, blk = q.shape
    assert n_blk % ATT_Q_BLOCKS == 0 and ATT_Q_BLOCKS % 2 == 0
    tq = ATT_Q_BLOCKS * blk
    spec = lambda rows: pl.BlockSpec((1, 1, n_blk, rows, blk), lambda b, h: (b, h, 0, 0, 0))
    return pl.pallas_call(
        _fox_kernel,
        grid=(batch, heads),
        in_specs=[spec(QK_ROWS), spec(QK_ROWS), spec(V_ROWS)],
        out_specs=spec(HEAD_DIM),
        out_shape=jax.ShapeDtypeStruct((batch, heads, n_blk, HEAD_DIM, blk), _BF16),
        scratch_shapes=[
            pltpu.VMEM((n_blk, blk, QK_ROWS), _BF16),
            pltpu.VMEM((2, blk, tq), _F32),
            pltpu.VMEM((2, blk, tq), _BF16),
            pltpu.VMEM((V_ROWS, tq), _F32),
        ],
        compiler_params=pltpu.CompilerParams(
            dimension_semantics=("arbitrary", "arbitrary"), vmem_limit_bytes=V7X_VMEM_LIMIT),
        name="fox_attention",
    )(q, k, v)


def _ffn_kernel(x_ref, a_ref, b_ref, wo_ref, g2_ref, wup_ref, wc_ref, bc_ref, wd_ref, g3_ref,
                o_ref, tail_ref):
    tm = x_ref.shape[1]
    n_chunks = wup_ref.shape[0]
    fc = wd_ref.shape[1]
    n_att = b_ref.shape[2]

    @pl.when(pl.program_id(1) == 0)
    def _():
        tail_ref[...] = jnp.zeros_like(tail_ref)

    mix_a = a_ref[0]
    mix_b = jnp.concatenate(
        [jnp.concatenate([b_ref[0, hd, j] for j in range(n_att)], axis=1)
         for hd in range(N_HEADS_B)], axis=0)
    mix = jnp.concatenate([mix_a, mix_b], axis=0)
    y = lax.dot_general(mix, wo_ref[...], _TN, preferred_element_type=_F32)
    x1 = x_ref[0] + y
    h = _rms_norm(x1, g2_ref[...]).astype(_BF16)

    first_rows = lax.broadcasted_iota(jnp.int32, (8, 2 * fc), 0)
    acc = jnp.zeros_like(x1)
    for c in range(n_chunks):
        a = jnp.dot(h, wup_ref[c], preferred_element_type=_F32)
        tail = tail_ref[c]
        tail_ref[c] = a[tm - 8:]
        prev1 = pltpu.roll(a, 1, 0)
        prev2 = pltpu.roll(a, 2, 0)
        head1 = jnp.where(first_rows < 1, pltpu.roll(tail, 1, 0), prev1[:8])
        head2 = jnp.where(first_rows < 2, pltpu.roll(tail, 2, 0), prev2[:8])
        prev1 = jnp.concatenate([head1, prev1[8:]], axis=0)
        prev2 = jnp.concatenate([head2, prev2[8:]], axis=0)
        wc = wc_ref[c]
        conv = prev2 * wc[0:1] + prev1 * wc[1:2] + a * wc[2:3] + bc_ref[c]
        gate, val = conv[:, :fc], conv[:, fc:]
        act = (gate * (1.0 / (1.0 + jnp.exp(-gate))) * val).astype(_BF16)
        acc = acc + jnp.dot(act, wd_ref[c], preferred_element_type=_F32)
    x2 = x1 + acc
    o_ref[0] = _rms_norm(x2, g3_ref[...]).astype(o_ref.dtype)


def _ffn(x, mix_a, mix_b, w_out, g2, w_up, w_conv, b_conv, w_down, g3):
    batch, seq, d_model = x.shape
    tm = FFN_TOKENS
    fc = FFN_CHUNK
    d_ff = w_down.shape[0]
    n_chunks = d_ff // fc
    d_a = mix_a.shape[1]

    def chunked(w):
        lead = w.shape[:-1]
        g = w[..., :d_ff].reshape(lead + (n_chunks, fc))
        v = w[..., d_ff:].reshape(lead + (n_chunks, fc))
        return jnp.moveaxis(jnp.concatenate([g, v], axis=-1), -2, 0)
    wup = chunked(w_up).astype(_BF16)
    wc = chunked(w_conv)
    bc = chunked(b_conv.reshape(1, -1))
    wd = w_down.reshape(n_chunks, fc, d_model).astype(_BF16)
    wo = w_out.astype(_BF16)

    n_att = tm // ATT_BLOCK
    const2 = lambda shape: pl.BlockSpec(shape, lambda b, s: (0, 0))
    const3 = lambda shape: pl.BlockSpec(shape, lambda b, s: (0, 0, 0))
    return pl.pallas_call(
        _ffn_kernel,
        grid=(batch, seq // tm),
        in_specs=[
            pl.BlockSpec((1, tm, d_model), lambda b, s: (b, s, 0)),
            pl.BlockSpec((1, d_a, tm), lambda b, s: (b, 0, s)),
            pl.BlockSpec((1, N_HEADS_B, n_att, HEAD_DIM, ATT_BLOCK), lambda b, s: (b, 0, s, 0, 0)),
            const2(wo.shape), const2((1, d_model)),
            const3(wup.shape), const3(wc.shape), const3(bc.shape), const3(wd.shape),
            const2((1, d_model)),
        ],
        out_specs=pl.BlockSpec((1, tm, d_model), lambda b, s: (b, s, 0)),
        out_shape=jax.ShapeDtypeStruct(x.shape, x.dtype),
        scratch_shapes=[pltpu.VMEM((n_chunks, 8, 2 * fc), _F32)],
        compiler_params=pltpu.CompilerParams(
            dimension_semantics=("arbitrary", "arbitrary"), vmem_limit_bytes=V7X_VMEM_LIMIT),
        name="out_proj_ffn",
    )(x, mix_a, mix_b, wo, g2.reshape(1, d_model), wup, wc, bc, wd, g3.reshape(1, d_model))


def kernel(x, norm_mix_g, w_in, f_bias, sg_ln_g, sg_w, sg_b, w_out, norm_ffn_g, w_up, w_conv,
           b_conv, w_down, norm_final_g):
    assert w_in.shape[0] == 1, "single trunk layer only"
    mix_a, q, k, v = _mixer_in(x, norm_mix_g[0], w_in[0], f_bias[0], sg_ln_g[0], sg_w[0], sg_b[0])
    mix_b = _fox_attention(q, k, v)
    return _ffn(x, mix_a, mix_b, w_out[0], norm_ffn_g[0], w_up[0], w_conv[0], b_conv[0], w_down[0],
                norm_final_g)
```

```python
import functools
import math

import jax
import jax.numpy as jnp
import numpy as np
from jax import lax
from jax.experimental import pallas as pl
from jax.experimental.pallas import tpu as pltpu

HEAD_DIM = 64
N_HEADS_A = 8
N_HEADS_B = 8
SG_BLOCK = 128
SG_CHUNK = 64
CONV_W = 3
EPS = 1e-6
LOG2_E = math.log2(math.e)
SKIP_LOG2 = 160.0
NORM_SLACK = 1.01

V7X_LANES = 128
BF16_SUBLANES = 16
V7X_VMEM_LIMIT = 56 * 1024 * 1024

PROJ_TOKENS = 512
FFN_TOKENS = 512
FFN_CHUNK = 256
ATT_BLOCK = 256
ATT_Q_BLOCKS = 4
V_ROWS = HEAD_DIM + BF16_SUBLANES
QK_ROWS = 2 * HEAD_DIM
GATE_ROWS = BF16_SUBLANES

_NT = (((1,), (1,)), ((), ()))
_TN = (((0,), (0,)), ((), ()))
_F32 = jnp.float32
_BF16 = jnp.bfloat16


def _split3(x):
    hi = x.astype(_BF16)
    r1 = x - hi.astype(_F32)
    mid = r1.astype(_BF16)
    lo = (r1 - mid.astype(_F32)).astype(_BF16)
    return hi, mid, lo


def _rms_norm(x, g):
    return x * lax.rsqrt(jnp.mean(x * x, axis=-1, keepdims=True) + EPS) * g


def _gelu(x):
    return 0.5 * x * (1.0 + lax.erf(x * (1.0 / math.sqrt(2.0))))


def _mixer_in_kernel(x_ref, g_ref, wuv_ref, wqk_ref, wvb_ref, wf_ref, fb_ref, lng_ref, wsg_ref,
                     bsg_ref, triu_ref, place_ref, outa_ref, q_ref, k_ref, v_ref, n2_ref, c2_ref, carry_ref):
    tm = x_ref.shape[1]
    n_sg = tm // SG_BLOCK
    n_att = tm // ATT_BLOCK
    d_a = N_HEADS_A * HEAD_DIM
    d_b = N_HEADS_B * HEAD_DIM

    @pl.when(pl.program_id(1) == 0)
    def _():
        carry_ref[...] = jnp.zeros_like(carry_ref)

    h = _rms_norm(x_ref[0], g_ref[...]).astype(_BF16)

    uv = lax.dot_general(wuv_ref[...], h, _NT, preferred_element_type=_F32)
    u = _gelu(uv[:d_a])
    v = _gelu(uv[d_a:]).reshape(N_HEADS_A, HEAD_DIM, tm)
    mu = jnp.mean(v, axis=1, keepdims=True)
    vc = v - mu
    var = jnp.mean(vc * vc, axis=1, keepdims=True)
    vn = (vc * lax.rsqrt(var + EPS)).reshape(d_a, tm) * lng_ref[...]
    vn = vn.astype(_BF16)

    row_chunk = lax.broadcasted_iota(jnp.int32, (SG_BLOCK, SG_BLOCK), 0) // SG_CHUNK
    col_chunk = lax.broadcasted_iota(jnp.int32, (SG_BLOCK, SG_BLOCK), 1) // SG_CHUNK
    sg_mask = row_chunk >= col_chunk
    for hd in range(N_HEADS_A):
        rows = slice(hd * HEAD_DIM, (hd + 1) * HEAD_DIM)
        w = jnp.where(sg_mask, wsg_ref[hd], 0.0).astype(_BF16)
        lhs = jnp.concatenate(
            [vn[rows, j * SG_BLOCK:(j + 1) * SG_BLOCK] for j in range(n_sg)], axis=0)
        mixed = lax.dot_general(lhs, w, _NT, preferred_element_type=_F32)
        mixed = mixed + bsg_ref[hd]
        for j in range(n_sg):
            cols = slice(j * SG_BLOCK, (j + 1) * SG_BLOCK)
            outa_ref[0, rows, cols] = (
                u[rows, cols] * mixed[j * HEAD_DIM:(j + 1) * HEAD_DIM]).astype(outa_ref.dtype)

    f = lax.dot_general(wf_ref[...], h, _NT, preferred_element_type=_F32) + fb_ref[...]
    ls = jnp.minimum(f, 0.0) - jnp.log1p(jnp.exp(-jnp.abs(f)))
    c = carry_ref[:, :1] + sum(
        jnp.dot(part, triu_ref[...], preferred_element_type=_F32) for part in _split3(ls))
    carry_ref[...] = jnp.broadcast_to(c[:, tm - 1:], carry_ref.shape)
    c2 = c * LOG2_E
    c2_ref[0] = c2
    c_parts = jnp.concatenate(_split3(c2) + (jnp.ones((GATE_ROWS, tm), _BF16),), axis=0)
    bias = jnp.dot(place_ref[...], c_parts, preferred_element_type=_F32)

    qk = lax.dot_general(wqk_ref[...], h, _NT, preferred_element_type=_F32)
    vb = lax.dot_general(wvb_ref[...], h, _NT, preferred_element_type=_F32)
    n2_ref[0] = jnp.sum((qk * qk).reshape(2 * N_HEADS_B, HEAD_DIM, tm), axis=1)
    ones_row = (lax.broadcasted_iota(jnp.int32, (V_ROWS - HEAD_DIM, ATT_BLOCK), 0) == 0).astype(v_ref.dtype)
    for hd in range(N_HEADS_B):
        rows = slice(hd * HEAD_DIM, (hd + 1) * HEAD_DIM)
        krows = slice(d_b + hd * HEAD_DIM, d_b + (hd + 1) * HEAD_DIM)
        for j in range(n_att):
            cols = slice(j * ATT_BLOCK, (j + 1) * ATT_BLOCK)
            q_ref[0, hd, j, :HEAD_DIM, :] = qk[rows, cols].astype(q_ref.dtype)
            q_ref[0, hd, j, HEAD_DIM:, :] = bias[rows, cols].astype(q_ref.dtype)
            k_ref[0, hd, j, :HEAD_DIM, :] = qk[krows, cols].astype(k_ref.dtype)
            k_ref[0, hd, j, HEAD_DIM:, :] = bias[krows, cols].astype(k_ref.dtype)
            v_ref[0, hd, j, :HEAD_DIM, :] = vb[rows, cols].astype(v_ref.dtype)
            v_ref[0, hd, j, HEAD_DIM:, :] = ones_row


def _mixer_in(x, norm_g, w_in, f_bias, sg_ln_g, sg_w, sg_b):
    batch, seq, d_model = x.shape
    tm = PROJ_TOKENS
    d_a = N_HEADS_A * HEAD_DIM
    d_b = N_HEADS_B * HEAD_DIM
    n_blk = seq // ATT_BLOCK
    scale = HEAD_DIM ** -0.5 * LOG2_E

    o = 0
    wuv = w_in[:, o:o + 2 * d_a].T.astype(_BF16); o += 2 * d_a
    wq = w_in[:, o:o + d_b] * scale; o += d_b
    wk = w_in[:, o:o + d_b]; o += d_b
    wqk = jnp.concatenate([wq, wk], axis=1).T.astype(_BF16)
    wvb = w_in[:, o:o + d_b].T.astype(_BF16); o += d_b
    pad_rows = ((0, GATE_ROWS - N_HEADS_B), (0, 0))
    wf = jnp.pad(w_in[:, o:o + N_HEADS_B].T, pad_rows).astype(_BF16)
    fb = jnp.broadcast_to(jnp.pad(f_bias.reshape(N_HEADS_B, 1), pad_rows), (GATE_ROWS, tm))
    lng = jnp.broadcast_to(sg_ln_g.reshape(d_a, 1), (d_a, tm))
    bsg = sg_b.reshape(N_HEADS_A, 1, SG_BLOCK)
    triu = jnp.asarray(np.triu(np.ones((tm, tm), np.float32)), _BF16)

    place = np.zeros((2 * d_b, 4 * GATE_ROWS), np.float32)
    ones_col = 3 * GATE_ROWS
    for hd in range(N_HEADS_B):
        qr, kr = hd * HEAD_DIM, d_b + hd * HEAD_DIM
        for t in range(3):
            place[qr + t, ones_col] = 1.0
            place[qr + 3 + t, t * GATE_ROWS + hd] = 1.0
            place[kr + t, t * GATE_ROWS + hd] = -1.0
            place[kr + 3 + t, ones_col] = 1.0
    place = jnp.asarray(place, _BF16)

    const2 = lambda shape: pl.BlockSpec(shape, lambda b, s: (0, 0))
    const3 = lambda shape: pl.BlockSpec(shape, lambda b, s: (0, 0, 0))
    blk5 = lambda rows: pl.BlockSpec((1, N_HEADS_B, tm // ATT_BLOCK, rows, ATT_BLOCK),
                                     lambda b, s: (b, 0, s, 0, 0))
    return pl.pallas_call(
        _mixer_in_kernel,
        grid=(batch, seq // tm),
        in_specs=[
            pl.BlockSpec((1, tm, d_model), lambda b, s: (b, s, 0)),
            const2((1, d_model)),
            const2(wuv.shape), const2(wqk.shape), const2(wvb.shape), const2(wf.shape),
            const2(fb.shape), const2(lng.shape), const3(sg_w.shape), const3(bsg.shape),
            const2(triu.shape), const2(place.shape),
        ],
        out_specs=[
            pl.BlockSpec((1, d_a, tm), lambda b, s: (b, 0, s)),
            blk5(QK_ROWS), blk5(QK_ROWS), blk5(V_ROWS),
            pl.BlockSpec((1, 2 * N_HEADS_B, tm), lambda b, s: (b, 0, s)),
            pl.BlockSpec((1, GATE_ROWS, tm), lambda b, s: (b, 0, s)),
        ],
        out_shape=[
            jax.ShapeDtypeStruct((batch, d_a, seq), _BF16),
            jax.ShapeDtypeStruct((batch, N_HEADS_B, n_blk, QK_ROWS, ATT_BLOCK), _BF16),
            jax.ShapeDtypeStruct((batch, N_HEADS_B, n_blk, QK_ROWS, ATT_BLOCK), _BF16),
            jax.ShapeDtypeStruct((batch, N_HEADS_B, n_blk, V_ROWS, ATT_BLOCK), _BF16),
            jax.ShapeDtypeStruct((batch, 2 * N_HEADS_B, seq), _F32),
            jax.ShapeDtypeStruct((batch, GATE_ROWS, seq), _F32),
        ],
        scratch_shapes=[pltpu.VMEM((GATE_ROWS, V7X_LANES), _F32)],
        compiler_params=pltpu.CompilerParams(
            dimension_semantics=("arbitrary", "arbitrary"), vmem_limit_bytes=V7X_VMEM_LIMIT),
        name="mixer_in",
    )(x, norm_g.reshape(1, d_model), wuv, wqk, wvb, wf, fb, lng, sg_w, bsg, triu, place)


def _fox_kernel(first_ref, q_ref, k_ref, v_ref, o_ref, ktok_ref, s_ref, p_ref, acc_ref):
    n_blk = q_ref.shape[2]
    blk = q_ref.shape[4]
    tq = ATT_Q_BLOCKS * blk

    def to_token_major(j, _):
        ktok_ref[j] = k_ref[0, 0, j].astype(_F32).T.astype(_BF16)
        return 0
    lax.fori_loop(0, n_blk, to_token_major, 0)

    s_minus_t = (lax.broadcasted_iota(jnp.int32, (blk, tq), 0)
                 - lax.broadcasted_iota(jnp.int32, (blk, tq), 1))

    n_q = n_blk // ATT_Q_BLOCKS
    first_base = (pl.program_id(0) * pl.num_programs(1) + pl.program_id(1)) * n_q

    def q_block(i, _):
        g0 = first_ref[first_base + i]
        q = jnp.concatenate([q_ref[0, 0, ATT_Q_BLOCKS * i + r] for r in range(ATT_Q_BLOCKS)], axis=1)
        n_keys = ATT_Q_BLOCKS * (i + 1)

        def scores(j, masked):
            s = jnp.dot(ktok_ref[j], q, preferred_element_type=_F32)
            if masked:
                s = jnp.where(s_minus_t <= (i * ATT_Q_BLOCKS - j) * blk, s, -jnp.inf)
            return s

        def step(j, slot, m, alpha_prev, masked):
            pv = jnp.dot(v_ref[0, 0, jnp.maximum(j - 1, 0)], p_ref[1 - slot], preferred_element_type=_F32)
            acc_ref[...] = alpha_prev * acc_ref[...] + pv
            s = s_ref[slot]
            m_new = jnp.maximum(m, jnp.max(s, axis=0, keepdims=True))
            alpha = jnp.exp2(m - m_new)
            p_ref[slot] = jnp.exp2(s - m_new).astype(_BF16)
            s_ref[1 - slot] = scores(jnp.minimum(j + 1, n_keys - 1), masked)
            return m_new, alpha

        def step_group(g, carry, masked):
            for r in range(ATT_Q_BLOCKS):
                carry = step(ATT_Q_BLOCKS * g + r, r % 2, *carry, masked)
            return carry

        s_ref[0] = scores(ATT_Q_BLOCKS * g0, True)
        p_ref[1] = jnp.zeros(p_ref.shape[1:], p_ref.dtype)
        acc_ref[...] = jnp.zeros_like(acc_ref)
        init = (jnp.full((1, tq), -1e30, _F32), jnp.ones((1, tq), _F32))
        n_plain = jnp.maximum(i - 1, g0)
        carry = lax.fori_loop(g0, n_plain, lambda g, c: step_group(g, c, False), init)
        _, alpha = lax.fori_loop(n_plain, i + 1, lambda g, c: step_group(g, c, True), carry)
        acc = alpha * acc_ref[...] + jnp.dot(v_ref[0, 0, n_keys - 1], p_ref[1], preferred_element_type=_F32)
        out = acc[:HEAD_DIM] / acc[HEAD_DIM:HEAD_DIM + 1]
        for r in range(ATT_Q_BLOCKS):
            o_ref[0, 0, ATT_Q_BLOCKS * i + r] = out[:, r * blk:(r + 1) * blk].astype(o_ref.dtype)
        return 0

    lax.fori_loop(0, n_q, q_block, 0)


def _first_key_group(n2, c2, blk):
    batch, _, seq = n2.shape
    tq = ATT_Q_BLOCKS * blk
    qn = jnp.sqrt(n2[:, :N_HEADS_B]) * NORM_SLACK
    kn = jnp.sqrt(n2[:, N_HEADS_B:]) * NORM_SLACK
    c2 = c2[:, :N_HEADS_B]
    per_key = lambda a: a.reshape(batch, N_HEADS_B, seq // blk, blk)
    per_query = lambda a: a.reshape(batch, N_HEADS_B, seq // tq, tq)
    k_max, c_min = per_key(kn).max(-1), per_key(c2).min(-1)
    q_max, c_max, d_max = per_query(qn).max(-1), per_query(c2).max(-1), per_query(qn * kn).max(-1)
    bound = (q_max[..., None] * k_max[..., None, :] + c_max[..., None] - c_min[..., None, :]
             + d_max[..., None])
    leading = jnp.cumprod((bound < -SKIP_LOG2).astype(jnp.int32), axis=-1).sum(-1)
    n_q = seq // tq
    first = jnp.minimum(leading // ATT_Q_BLOCKS, jnp.arange(n_q, dtype=jnp.int32))
    return first.reshape(-1).astype(jnp.int32)


def _fox_attention(q, k, v, n2, c2):
    batch, heads, n_blk, _, blk = q.shape
    assert n_blk % ATT_Q_BLOCKS == 0 and ATT_Q_BLOCKS % 2 == 0
    tq = ATT_Q_BLOCKS * blk
    first = _first_key_group(n2, c2, blk)
    spec = lambda rows: pl.BlockSpec((1, 1, n_blk, rows, blk), lambda b, h, first: (b, h, 0, 0, 0))
    return pl.pallas_call(
        _fox_kernel,
        grid_spec=pltpu.PrefetchScalarGridSpec(
            num_scalar_prefetch=1,
            grid=(batch, heads),
            in_specs=[spec(QK_ROWS), spec(QK_ROWS), spec(V_ROWS)],
            out_specs=spec(HEAD_DIM),
            scratch_shapes=[
                pltpu.VMEM((n_blk, blk, QK_ROWS), _BF16),
                pltpu.VMEM((2, blk, tq), _F32),
                pltpu.VMEM((2, blk, tq), _BF16),
                pltpu.VMEM((V_ROWS, tq), _F32),
            ],
        ),
        out_shape=jax.ShapeDtypeStruct((batch, heads, n_blk, HEAD_DIM, blk), _BF16),
        compiler_params=pltpu.CompilerParams(
            dimension_semantics=("arbitrary", "arbitrary"), vmem_limit_bytes=V7X_VMEM_LIMIT),
        name="fox_attention",
    )(first, q, k, v)


def _ffn_kernel(x_ref, a_ref, b_ref, wo_ref, g2_ref, wup_ref, wc_ref, bc_ref, wd_ref, g3_ref,
                o_ref, tail_ref):
    tm = x_ref.shape[1]
    n_chunks = wup_ref.shape[0]
    fc = wd_ref.shape[1]
    n_att = b_ref.shape[2]

    @pl.when(pl.program_id(1) == 0)
    def _():
        tail_ref[...] = jnp.zeros_like(tail_ref)

    mix_a = a_ref[0]
    mix_b = jnp.concatenate(
        [jnp.concatenate([b_ref[0, hd, j] for j in range(n_att)], axis=1)
         for hd in range(N_HEADS_B)], axis=0)
    mix = jnp.concatenate([mix_a, mix_b], axis=0)
    y = lax.dot_general(mix, wo_ref[...], _TN, preferred_element_type=_F32)
    x1 = x_ref[0] + y
    h = _rms_norm(x1, g2_ref[...]).astype(_BF16)

    sub = lax.broadcasted_iota(jnp.int32, (1, 8, 2 * fc), 1)
    acc = jnp.zeros_like(x1)
    for c in range(n_chunks):
        a = jnp.dot(h, wup_ref[c], preferred_element_type=_F32)
        a = a.reshape(tm // 8, 8, 2 * fc)
        tail = tail_ref[c][None]
        tail_ref[c] = a[tm // 8 - 1]
        rot1, rot2 = pltpu.roll(a, 1, 1), pltpu.roll(a, 2, 1)
        before1 = jnp.concatenate([pltpu.roll(tail, 1, 1), rot1[:-1]], axis=0)
        before2 = jnp.concatenate([pltpu.roll(tail, 2, 1), rot2[:-1]], axis=0)
        prev1 = jnp.where(sub < 1, before1, rot1)
        prev2 = jnp.where(sub < 2, before2, rot2)
        wc = wc_ref[c]
        conv = prev2 * wc[0:1] + prev1 * wc[1:2] + a * wc[2:3] + bc_ref[c]
        conv = conv.reshape(tm, 2 * fc)
        gate, val = conv[:, :fc], conv[:, fc:]
        act = (gate * (1.0 / (1.0 + jnp.exp(-gate))) * val).astype(_BF16)
        acc = acc + jnp.dot(act, wd_ref[c], preferred_element_type=_F32)
    x2 = x1 + acc
    o_ref[0] = _rms_norm(x2, g3_ref[...]).astype(o_ref.dtype)


def _ffn(x, mix_a, mix_b, w_out, g2, w_up, w_conv, b_conv, w_down, g3):
    batch, seq, d_model = x.shape
    tm = FFN_TOKENS
    fc = FFN_CHUNK
    d_ff = w_down.shape[0]
    n_chunks = d_ff // fc
    d_a = mix_a.shape[1]

    def chunked(w):
        lead = w.shape[:-1]
        g = w[..., :d_ff].reshape(lead + (n_chunks, fc))
        v = w[..., d_ff:].reshape(lead + (n_chunks, fc))
        return jnp.moveaxis(jnp.concatenate([g, v], axis=-1), -2, 0)
    wup = chunked(w_up).astype(_BF16)
    wc = chunked(w_conv)
    bc = chunked(b_conv.reshape(1, -1))
    wd = w_down.reshape(n_chunks, fc, d_model).astype(_BF16)
    wo = w_out.astype(_BF16)

    n_att = tm // ATT_BLOCK
    const2 = lambda shape: pl.BlockSpec(shape, lambda b, s: (0, 0))
    const3 = lambda shape: pl.BlockSpec(shape, lambda b, s: (0, 0, 0))
    return pl.pallas_call(
        _ffn_kernel,
        grid=(batch, seq // tm),
        in_specs=[
            pl.BlockSpec((1, tm, d_model), lambda b, s: (b, s, 0)),
            pl.BlockSpec((1, d_a, tm), lambda b, s: (b, 0, s)),
            pl.BlockSpec((1, N_HEADS_B, n_att, HEAD_DIM, ATT_BLOCK), lambda b, s: (b, 0, s, 0, 0)),
            const2(wo.shape), const2((1, d_model)),
            const3(wup.shape), const3(wc.shape), const3(bc.shape), const3(wd.shape),
            const2((1, d_model)),
        ],
        out_specs=pl.BlockSpec((1, tm, d_model), lambda b, s: (b, s, 0)),
        out_shape=jax.ShapeDtypeStruct(x.shape, x.dtype),
        scratch_shapes=[pltpu.VMEM((n_chunks, 8, 2 * fc), _F32)],
        compiler_params=pltpu.CompilerParams(
            dimension_semantics=("arbitrary", "arbitrary"), vmem_limit_bytes=V7X_VMEM_LIMIT),
        name="out_proj_ffn",
    )(x, mix_a, mix_b, wo, g2.reshape(1, d_model), wup, wc, bc, wd, g3.reshape(1, d_model))


def kernel(x, norm_mix_g, w_in, f_bias, sg_ln_g, sg_w, sg_b, w_out, norm_ffn_g, w_up, w_conv,
           b_conv, w_down, norm_final_g):
    assert w_in.shape[0] == 1, "single trunk layer only"
    mix_a, q, k, v, n2, c2 = _mixer_in(x, norm_mix_g[0], w_in[0], f_bias[0], sg_ln_g[0], sg_w[0], sg_b[0])
    mix_b = _fox_attention(q, k, v, n2, c2)
    return _ffn(x, mix_a, mix_b, w_out[0], norm_ffn_g[0], w_up[0], w_conv[0], b_conv[0], w_down[0],
                norm_final_g)
```
